```python
import math
import jax, jax.numpy as jnp
from jax import lax
import numpy as np

D_MODEL = 4096
BATCH = 4
SEQ = 2048
DEPTH = 2
DEC_BATCH = 128
DEC_SEQ = 1
PAST_LEN = 16384
PAGE_SIZE = 128

DN_HEADS = 16
DN_DK = 128
DN_DV = 128
DN_QK = DN_HEADS * DN_DK
DN_VW = DN_HEADS * DN_DV
DN_CONV_CH = 2 * DN_QK + DN_VW
SHORT_CONV = 4
CHUNK = 64
CF_CH = D_MODEL // 2
CF_WIDTH = 31
D_MIX = DN_VW + CF_CH
OFF_Z = DN_CONV_CH
OFF_BETA = OFF_Z + DN_VW
OFF_A = OFF_BETA + DN_HEADS
OFF_GLU = OFF_A + DN_HEADS
D_IN = OFF_GLU + 2 * CF_CH
PEER_HEADS = 8
PEER_NKEYS = 128
PEER_EXPERTS = PEER_NKEYS ** 2
PEER_DQ = 256
PEER_HALF = PEER_DQ // 2
PEER_TOPK = 16
PEER_BLOCK = 64
N_MOD = 6
EPS = 1e-6
F32 = jnp.float32

kernel_name = 'hymba_gdn_conformer_peer_step'


def rms_norm(x, w):
    xf = x.astype(F32)
    y = xf * lax.rsqrt(jnp.mean(xf * xf, axis=-1, keepdims=True) + EPS)
    return (y * w.astype(F32)).astype(x.dtype)


def layer_norm(x, g, b):
    xf = x.astype(F32)
    mu = jnp.mean(xf, axis=-1, keepdims=True)
    xc = xf - mu
    y = xc * lax.rsqrt(jnp.mean(xc * xc, axis=-1, keepdims=True) + EPS)
    return (y * g.astype(F32) + b.astype(F32)).astype(x.dtype)


def l2_normalize(x):
    xf = x.astype(F32)
    return xf * lax.rsqrt(jnp.sum(xf * xf, axis=-1, keepdims=True) + EPS)


def causal_dwconv(x_full, w):
    return lax.conv_general_dilated(
        x_full, w[:, None, :].astype(x_full.dtype), window_strides=(1,), padding='VALID',
        dimension_numbers=('NWC', 'WIO', 'NWC'), feature_group_count=x_full.shape[-1])


def delta_rule_chunked(q, k, v, g, beta, s0):
    b, t, h, dk = q.shape
    dv = v.shape[-1]
    n = t // CHUNK

    def to_chunks(x):
        return jnp.moveaxis(x.reshape((b, n, CHUNK, h) + x.shape[3:]), 3, 1)

    q, k, v, g, beta = (to_chunks(z) for z in (q, k, v, g, beta))
    gc = jnp.cumsum(g, axis=-1)
    causal = jnp.tril(jnp.ones((CHUNK, CHUNK), bool))
    strict = jnp.tril(jnp.ones((CHUNK, CHUNK), bool), -1)
    decay = jnp.exp(jnp.where(causal, gc[..., :, None] - gc[..., None, :], -jnp.inf))
    k_beta = k * beta[..., None]
    m = jnp.where(strict, jnp.einsum('bhnid,bhnjd->bhnij', k_beta, k) * decay, 0.0)
    rhs = jnp.concatenate([k_beta * jnp.exp(gc)[..., None], v * beta[..., None]], axis=-1)
    sol = lax.linalg.triangular_solve(jnp.eye(CHUNK, dtype=F32) + m, rhs,
                                      left_side=True, lower=True, unit_diagonal=True)
    w, u = sol[..., :dk], sol[..., dk:]
    attn = jnp.einsum('bhnid,bhnjd->bhnij', q, k) * decay
    q_dec = q * jnp.exp(gc)[..., None]
    k_dec = k * jnp.exp(gc[..., -1:] - gc)[..., None]
    g_last = jnp.exp(gc[..., -1])

    def step(s, xs):
        w_c, u_c, a_c, qd_c, kd_c, gl_c = xs
        v_new = u_c - jnp.einsum('bhcd,bhde->bhce', w_c, s)
        o_c = jnp.einsum('bhcd,bhde->bhce', qd_c, s) + jnp.einsum('bhij,bhje->bhie', a_c, v_new)
        s = s * gl_c[..., None, None] + jnp.einsum('bhcd,bhce->bhde', kd_c, v_new)
        return s, o_c

    xs = tuple(jnp.moveaxis(z, 2, 0) for z in (w, u, attn, q_dec, k_dec, g_last))
    s_final, o = lax.scan(step, s0, xs)
    o = jnp.transpose(o, (1, 0, 3, 2, 4)).reshape(b, t, h, dv)
    return o, s_final


def delta_rule_recurrent(q, k, v, g, beta, s0):
    def step(s, xs):
        q_t, k_t, v_t, g_t, b_t = xs
        s = s * jnp.exp(g_t)[..., None, None]
        delta = (v_t - jnp.einsum('bhd,bhde->bhe', k_t, s)) * b_t[..., None]
        s = s + jnp.einsum('bhd,bhe->bhde', k_t, delta)
        return s, jnp.einsum('bhd,bhde->bhe', q_t, s)

    xs = tuple(jnp.moveaxis(z, 1, 0) for z in (q, k, v, g, beta))
    s_final, o = lax.scan(step, s0, xs)
    return jnp.moveaxis(o, 0, 1), s_final


def token_mixer(h, dn_buf, dn_s, cf_buf, w_in, w_short, a_log, dt_bias, dn_norm,
                w_cf, b_cf, cf_g, cf_b, w_out, chunked):
    bsz, t, _ = h.shape
    proj = h @ w_in
    qkv_full = jnp.concatenate([dn_buf.astype(proj.dtype), proj[..., :OFF_Z]], axis=1)
    new_dn_buf = qkv_full[:, -(SHORT_CONV - 1):]
    qkv = jax.nn.silu(causal_dwconv(qkv_full, w_short))
    q = l2_normalize(qkv[..., :DN_QK].reshape(bsz, t, DN_HEADS, DN_DK)) * (DN_DK ** -0.5)
    k = l2_normalize(qkv[..., DN_QK:2 * DN_QK].reshape(bsz, t, DN_HEADS, DN_DK))
    v = qkv[..., 2 * DN_QK:].reshape(bsz, t, DN_HEADS, DN_DV).astype(F32)
    z = proj[..., OFF_Z:OFF_BETA].reshape(bsz, t, DN_HEADS, DN_DV).astype(F32)
    beta = jax.nn.sigmoid(proj[..., OFF_BETA:OFF_A].astype(F32))
    g = -jnp.exp(a_log.astype(F32)) * jax.nn.softplus(
        proj[..., OFF_A:OFF_GLU].astype(F32) + dt_bias.astype(F32))
    core = delta_rule_chunked if chunked else delta_rule_recurrent
    o, new_s = core(q, k, v, g, beta, dn_s.astype(F32))
    o_dn = (rms_norm(o, dn_norm) * jax.nn.silu(z)).reshape(bsz, t, DN_VW).astype(h.dtype)
    glu_in = proj[..., OFF_GLU:]
    glu = glu_in[..., :CF_CH] * jax.nn.sigmoid(glu_in[..., CF_CH:])
    cf_full = jnp.concatenate([cf_buf.astype(glu.dtype), glu], axis=1)
    new_cf_buf = cf_full[:, -(CF_WIDTH - 1):]
    y = causal_dwconv(cf_full, w_cf) + b_cf
    o_cf = jax.nn.silu(layer_norm(y, cf_g, cf_b))
    out = jnp.concatenate([o_dn, o_cf], axis=-1) @ w_out
    return out, new_dn_buf, new_s.astype(dn_s.dtype), new_cf_buf


def peer_ffn(h, w_q, sub_keys, u_tab, v_tab):
    bsz, t, d = h.shape
    n_tok = bsz * t
    n_pad = (-n_tok) % PEER_BLOCK
    blocks = jnp.pad(h.reshape(n_tok, d), ((0, n_pad), (0, 0))).reshape(-1, PEER_BLOCK, d)

    def block_fn(xb):
        qh = (xb @ w_q).reshape(PEER_BLOCK, PEER_HEADS, 2, PEER_HALF)
        s = jnp.einsum('thpd,hpkd->thpk', qh, sub_keys).astype(F32)
        s_top, i_top = lax.top_k(s, PEER_TOPK)
        cand_s = s_top[:, :, 0, :, None] + s_top[:, :, 1, None, :]
        cand_i = i_top[:, :, 0, :, None] * PEER_NKEYS + i_top[:, :, 1, None, :]
        best_s, pos = lax.top_k(cand_s.reshape(PEER_BLOCK, PEER_HEADS, PEER_TOPK * PEER_TOPK), PEER_TOPK)
        idx = jnp.take_along_axis(cand_i.reshape(PEER_BLOCK, PEER_HEADS, -1), pos, axis=-1)
        gate = jax.nn.softmax(best_s, axis=-1)
        act = jax.nn.gelu(jnp.einsum('thkd,td->thk', jnp.take(u_tab, idx, axis=0), xb).astype(F32),
                          approximate=False)
        coef = (gate * act).astype(xb.dtype)
        return jnp.einsum('thk,thkd->td', coef, jnp.take(v_tab, idx, axis=0))

    out = lax.map(block_fn, blocks)
    return out.reshape(-1, d)[:n_tok].reshape(bsz, t, d)


def run_trunk(x, c, dn_bufs, dn_states, cf_bufs, weights, chunked):
    (norm_mix, norm_ffn, norm_final, w_ada, b_ada, w_in, w_short_conv, a_log, dt_bias, dn_norm,
     w_cf_dw, b_cf_dw, cf_ln_g, cf_ln_b, w_out, peer_w_q, peer_sub_keys, peer_u, peer_v) = weights
    silu_c = jax.nn.silu(c)
    out_dn_buf, out_dn_s, out_cf_buf = [], [], []
    for l in range(DEPTH):
        mod = (silu_c @ w_ada[l] + b_ada[l])[:, None, :]
        shift1, scale1, gate1, shift2, scale2, gate2 = jnp.split(mod, N_MOD, axis=-1)
        h = rms_norm(x, norm_mix[l]) * (1 + scale1) + shift1
        mix, nb, ns, nc = token_mixer(h, dn_bufs[l], dn_states[l], cf_bufs[l], w_in[l], w_short_conv[l],
                                      a_log[l], dt_bias[l], dn_norm[l], w_cf_dw[l], b_cf_dw[l],
                                      cf_ln_g[l], cf_ln_b[l], w_out[l], chunked)
        x = x + gate1 * mix
        h = rms_norm(x, norm_ffn[l]) * (1 + scale2) + shift2
        x = x + gate2 * peer_ffn(h, peer_w_q[l], peer_sub_keys[l], peer_u[l], peer_v[l])
        out_dn_buf.append(nb)
        out_dn_s.append(ns)
        out_cf_buf.append(nc)
    y = rms_norm(x, norm_final)
    return y, jnp.stack(out_dn_buf), jnp.stack(out_dn_s), jnp.stack(out_cf_buf)


def setup_inputs(seed: int = 0) -> dict:
    key = jax.random.key(seed)
    ks = jax.random.split(key, 32)

    def nrm(k, shape, s):
        return jax.random.normal(k, shape, F32) * s

    dt = jnp.exp(jax.random.uniform(ks[15], (DEPTH, DN_HEADS), F32, math.log(1e-3), math.log(1e-1)))
    return {
        'x_prompt': nrm(ks[0], (BATCH, SEQ, D_MODEL), 1.0),
        'x_sample': nrm(ks[1], (DEC_BATCH, DEC_SEQ, D_MODEL), 1.0),
        'state_dn_conv': nrm(ks[2], (DEPTH, DEC_BATCH, SHORT_CONV - 1, DN_CONV_CH), 1.0),
        'state_dn_ssm': nrm(ks[3], (DEPTH, DEC_BATCH, DN_HEADS, DN_DK, DN_DV), 0.05),
        'state_cf_conv': nrm(ks[4], (DEPTH, DEC_BATCH, CF_WIDTH - 1, CF_CH), 0.5),
        'c_prompt': nrm(ks[5], (BATCH, D_MODEL), 1.0),
        'c_sample': nrm(ks[6], (DEC_BATCH, D_MODEL), 1.0),
        'norm_mix': 1.0 + nrm(ks[7], (DEPTH, D_MODEL), 0.02),
        'norm_ffn': 1.0 + nrm(ks[8], (DEPTH, D_MODEL), 0.02),
        'norm_final': 1.0 + nrm(ks[9], (D_MODEL,), 0.02),
        'w_ada': nrm(ks[10], (DEPTH, D_MODEL, N_MOD * D_MODEL), 0.5 * D_MODEL ** -0.5),
        'b_ada': nrm(ks[11], (DEPTH, N_MOD * D_MODEL), 0.02),
        'w_in': nrm(ks[12], (DEPTH, D_MODEL, D_IN), D_MODEL ** -0.5),
        'w_short_conv': nrm(ks[13], (DEPTH, SHORT_CONV, DN_CONV_CH), SHORT_CONV ** -0.5),
        'a_log': jnp.log(jax.random.uniform(ks[14], (DEPTH, DN_HEADS), F32, 1.0, 16.0)),
        'dt_bias': dt + jnp.log(-jnp.expm1(-dt)),
        'dn_norm': 1.0 + nrm(ks[16], (DEPTH, DN_DV), 0.02),
        'w_cf_dw': nrm(ks[17], (DEPTH, CF_WIDTH, CF_CH), CF_WIDTH ** -0.5),
        'b_cf_dw': nrm(ks[18], (DEPTH, CF_CH), 0.02),
        'cf_ln_g': 1.0 + nrm(ks[19], (DEPTH, CF_CH), 0.02),
        'cf_ln_b': nrm(ks[20], (DEPTH, CF_CH), 0.02),
        'w_out': nrm(ks[21], (DEPTH, D_MIX, D_MODEL), D_MIX ** -0.5),
        'peer_w_q': nrm(ks[22], (DEPTH, D_MODEL, PEER_HEADS * PEER_DQ), D_MODEL ** -0.5),
        'peer_sub_keys': nrm(ks[23], (DEPTH, PEER_HEADS, 2, PEER_NKEYS, PEER_HALF), PEER_HALF ** -0.5),
        'peer_u': nrm(ks[24], (DEPTH, PEER_EXPERTS, D_MODEL), D_MODEL ** -0.5),
        'peer_v': nrm(ks[25], (DEPTH, PEER_EXPERTS, D_MODEL), PEER_HEADS ** -0.5),
    }


def reference(x_prompt, x_sample, state_dn_conv, state_dn_ssm, state_cf_conv, c_prompt, c_sample,
              norm_mix, norm_ffn, norm_final, w_ada, b_ada, w_in, w_short_conv, a_log, dt_bias,
              dn_norm, w_cf_dw, b_cf_dw, cf_ln_g, cf_ln_b, w_out, peer_w_q, peer_sub_keys,
              peer_u, peer_v):
    weights = (norm_mix, norm_ffn, norm_final, w_ada, b_ada, w_in, w_short_conv, a_log, dt_bias,
               dn_norm, w_cf_dw, b_cf_dw, cf_ln_g, cf_ln_b, w_out, peer_w_q, peer_sub_keys,
               peer_u, peer_v)
    bp = x_prompt.shape[0]
    zero_dn_buf = jnp.zeros((DEPTH, bp, SHORT_CONV - 1, DN_CONV_CH), x_prompt.dtype)
    zero_dn_s = jnp.zeros((DEPTH, bp, DN_HEADS, DN_DK, DN_DV), x_prompt.dtype)
    zero_cf_buf = jnp.zeros((DEPTH, bp, CF_WIDTH - 1, CF_CH), x_prompt.dtype)
    y_prompt, dn_conv_p, dn_ssm_p, cf_conv_p = run_trunk(
        x_prompt, c_prompt, zero_dn_buf, zero_dn_s, zero_cf_buf, weights, True)
    y_sample, dn_conv_s, dn_ssm_s, cf_conv_s = run_trunk(
        x_sample, c_sample, state_dn_conv, state_dn_ssm, state_cf_conv, weights, False)
    return (y_prompt, y_sample, dn_conv_p, dn_ssm_p, cf_conv_p, dn_conv_s, dn_ssm_s, cf_conv_s)
```

```python
import functools
import math

import jax
import jax.numpy as jnp
from jax import lax
from jax.experimental import pallas as pl
from jax.experimental.pallas import tpu as pltpu

F32 = jnp.float32
BF16 = jnp.bfloat16
EPS = 1e-6
HIGHEST = lax.Precision.HIGHEST
SQRT_HALF = 0.7071067811865476

LANES = 128
SUBLANES = 8
VMEM_LIMIT_BYTES = 56 * 1024 * 1024

DN_HEADS = 16
DN_DK = 128
DN_DV = 128
DN_QK = DN_HEADS * DN_DK
DN_VW = DN_HEADS * DN_DV
DN_CONV_CH = 2 * DN_QK + DN_VW
SHORT_CONV = 4
CHUNK = 64
CF_WIDTH = 31
N_MOD = 6
PEER_HEADS = 8
PEER_NKEYS = 128
PEER_HALF = 128
PEER_TOPK = 16


def _params(*sem):
    return pltpu.CompilerParams(dimension_semantics=sem, vmem_limit_bytes=VMEM_LIMIT_BYTES)


def _dot(a, b):
    return jnp.dot(a, b, preferred_element_type=F32)


def _dot_hi(a, b):
    return jnp.dot(a, b, preferred_element_type=F32, precision=HIGHEST)


def _dot_nt(a, b, precision=None):
    return lax.dot_general(a, b, (((1,), (1,)), ((), ())), preferred_element_type=F32, precision=precision)


def _dot_tn(a, b, precision=None):
    return lax.dot_general(a, b, (((0,), (0,)), ((), ())), preferred_element_type=F32, precision=precision)


def _silu(x):
    return x * jax.nn.sigmoid(x)


def _ada_kernel(c_ref, w_ref, b_ref, o_ref):
    c = c_ref[...]
    o_ref[0] = _dot(_silu(c).astype(BF16), w_ref[0].astype(BF16)) + b_ref[0]


def _ada(c_all, w_ada, b_ada, tn=512):
    depth, d, n = w_ada.shape
    r = c_all.shape[0]
    return pl.pallas_call(
        _ada_kernel,
        grid=(depth, n // tn),
        in_specs=[
            pl.BlockSpec((r, d), lambda l, j: (0, 0)),
            pl.BlockSpec((1, d, tn), lambda l, j: (l, 0, j)),
            pl.BlockSpec((1, 1, tn), lambda l, j: (l, 0, j)),
        ],
        out_specs=pl.BlockSpec((1, r, tn), lambda l, j: (l, 0, j)),
        out_shape=jax.ShapeDtypeStruct((depth, r, n), F32),
        compiler_params=_params("parallel", "parallel"),
        name="ada",
    )(c_all, w_ada, b_ada.reshape(depth, 1, n))


def _norm_kernel(*refs, has_res, has_mod):
    it = iter(refs)
    x = next(it)[0]
    if has_res:
        p_ref, g_ref = next(it), next(it)
        x = x + g_ref[0] * p_ref[0]
    w_ref = next(it)
    if has_mod:
        sc_ref, sh_ref = next(it), next(it)
    if has_res:
        next(it)[0] = x
    y = x * lax.rsqrt(jnp.mean(x * x, axis=-1, keepdims=True) + EPS) * w_ref[...]
    if has_mod:
        y = y * (1.0 + sc_ref[0]) + sh_ref[0]
    o_ref = next(it)
    o_ref[0] = y.astype(o_ref.dtype)


def _norm(x, w, mod=None, mod_cols=None, res=None, res_mod=None, res_gate_col=None, out_dtype=BF16, t_tile=256):
    g, t, d = x.shape
    tt = min(t_tile, t)

    def mod_spec(m, col):
        if m.shape[1] == t and t > 1:
            return pl.BlockSpec((1, tt, d), lambda b, i: (b, i, col))
        return pl.BlockSpec((1, 1, d), lambda b, i: (b, 0, col))

    x_spec = pl.BlockSpec((1, tt, d), lambda b, i: (b, i, 0))
    args, specs = [x], [x_spec]
    if res is not None:
        args += [res, res_mod]
        specs += [x_spec, mod_spec(res_mod, res_gate_col)]
    args.append(w.reshape(1, d))
    specs.append(pl.BlockSpec((1, d), lambda b, i: (0, 0)))
    if mod_cols is not None:
        args += [mod, mod]
        specs += [mod_spec(mod, mod_cols[0]), mod_spec(mod, mod_cols[1])]
    out_shapes, out_specs = [], []
    if res is not None:
        out_shapes.append(jax.ShapeDtypeStruct((g, t, d), F32))
        out_specs.append(x_spec)
    out_shapes.append(jax.ShapeDtypeStruct((g, t, d), out_dtype))
    out_specs.append(x_spec)
    outs = pl.pallas_call(
        functools.partial(_norm_kernel, has_res=res is not None, has_mod=mod_cols is not None),
        grid=(g, t // tt),
        in_specs=specs,
        out_specs=out_specs,
        out_shape=out_shapes,
        compiler_params=_params("parallel", "parallel"),
        name="norm",
    )(*args)
    return outs if res is not None else outs[0]


def _mm_kernel(a_ref, w_ref, o_ref):
    o_ref[0] = _dot(a_ref[0], w_ref[...]).astype(o_ref.dtype)


def _matmul(a, w, tt=1024, tn=1024, out_dtype=F32):
    g, t, k = a.shape
    n = w.shape[1]
    tt, tn = min(tt, t), min(tn, n)
    return pl.pallas_call(
        _mm_kernel,
        grid=(g, t // tt, n // tn),
        in_specs=[
            pl.BlockSpec((1, tt, k), lambda b, i, j: (b, i, 0)),
            pl.BlockSpec((k, tn), lambda b, i, j: (0, j)),
        ],
        out_specs=pl.BlockSpec((1, tt, tn), lambda b, i, j: (b, i, j)),
        out_shape=jax.ShapeDtypeStruct((g, t, n), out_dtype),
        compiler_params=_params("parallel", "parallel", "arbitrary"),
        name="matmul",
    )(a, w)


def _mm_out_kernel(a1_ref, a2_ref, w1_ref, w2_ref, x_ref, g_ref, o_ref):
    mix = _dot(a1_ref[0], w1_ref[...]) + _dot(a2_ref[0], w2_ref[...])
    o_ref[0] = x_ref[0] + g_ref[0] * mix


def _matmul_out(a1, a2, w, x, mod, gate_col, tt=1024, tn=1024):
    g, t, k = a1.shape
    n = w.shape[1]
    tt, tn = min(tt, t), min(tn, n)
    per_token = mod.shape[1] == t and t > 1
    cols = n // tn
    if per_token:
        g_spec = pl.BlockSpec((1, tt, tn), lambda b, i, j: (b, i, gate_col * cols + j))
    else:
        g_spec = pl.BlockSpec((1, 1, tn), lambda b, i, j: (b, 0, gate_col * cols + j))
    a_spec = pl.BlockSpec((1, tt, k), lambda b, i, j: (b, i, 0))
    return pl.pallas_call(
        _mm_out_kernel,
        grid=(g, t // tt, n // tn),
        in_specs=[
            a_spec, a_spec,
            pl.BlockSpec((k, tn), lambda b, i, j: (0, j)),
            pl.BlockSpec((k, tn), lambda b, i, j: (1, j)),
            pl.BlockSpec((1, tt, tn), lambda b, i, j: (b, i, j)),
            g_spec,
        ],
        out_specs=pl.BlockSpec((1, tt, tn), lambda b, i, j: (b, i, j)),
        out_shape=jax.ShapeDtypeStruct((g, t, n), F32),
        compiler_params=_params("parallel", "parallel", "arbitrary"),
        name="matmul_out",
    )(a1, a2, w, w, x, mod)


def _softplus(x):
    return jnp.maximum(x, 0.0) + jnp.log1p(jnp.exp(-jnp.abs(x)))


def _gates(ba, gp):
    beta = jax.nn.sigmoid(ba)
    g = -jnp.exp(gp[0:1]) * _softplus(ba + gp[1:2])
    return beta, g


def _unit_lower_inverse(m):
    c = m.shape[0]
    r = lax.broadcasted_iota(jnp.int32, (c, c), 0)
    q = lax.broadcasted_iota(jnp.int32, (c, c), 1)
    blk16 = (r >> 4) == (q >> 4)
    blk32 = (r >> 5) == (q >> 5)
    eye = jnp.where(r == q, 1.0, 0.0).astype(F32)
    n = jnp.where(blk16, -m, 0.0)
    t = eye + n
    p = n
    for _ in range(3):
        p = _dot_hi(p, p)
        t = t + _dot_hi(t, p)
    c32 = jnp.where(blk16, 0.0, jnp.where(blk32, m, 0.0))
    t = t - _dot_hi(_dot_hi(t, c32), t)
    c64 = jnp.where(blk32, 0.0, m)
    t = t - _dot_hi(_dot_hi(t, c64), t)
    return t


def _dn_prompt_kernel(q_ref, k_ref, v_ref, z_ref, ba_ref, wq_ref, wk_ref, wv_ref, gp_ref, nw_ref,
                      o_ref, s_ref):
    h = pl.program_id(1)
    n_chunks = q_ref.shape[1] // CHUNK
    c = CHUNK
    lane = lax.broadcasted_iota(jnp.int32, (c, LANES), 1)
    r = lax.broadcasted_iota(jnp.int32, (c, c), 0)
    q_i = lax.broadcasted_iota(jnp.int32, (c, c), 1)
    causal = r >= q_i
    strict = r > q_i
    tril = jnp.where(causal, 1.0, 0.0).astype(F32)
    eye = r == q_i
    gp = gp_ref[...]
    nw = nw_ref[...]

    def conv(ref, w_ref, t0, tp, keep):
        prev = ref[0, pl.ds(tp, SUBLANES), :] * keep
        cur = ref[0, pl.ds(t0, c), :]
        blk = jnp.concatenate([prev, cur], axis=0)
        w = w_ref[...]
        y = (w[3:4] * blk[8:8 + c] + w[2:3] * blk[7:7 + c] + w[1:2] * blk[6:6 + c] + w[0:1] * blk[5:5 + c])
        return _silu(y)

    def l2n(x):
        return x * lax.rsqrt(jnp.sum(x * x, axis=-1, keepdims=True) + EPS)

    def chunk(ci, s):
        t0 = pl.multiple_of(ci * c, c)
        tp = pl.multiple_of(jnp.maximum(t0 - SUBLANES, 0), SUBLANES)
        keep = jnp.where(ci > 0, 1.0, 0.0).astype(F32)
        q = l2n(conv(q_ref, wq_ref, t0, tp, keep)) * (DN_DK ** -0.5)
        k = l2n(conv(k_ref, wk_ref, t0, tp, keep))
        v = conv(v_ref, wv_ref, t0, tp, keep)
        beta_all, g_all = _gates(ba_ref[0, pl.ds(t0, c), :], gp)
        gc_all = _dot_hi(tril, g_all)
        beta = jnp.sum(jnp.where(lane == h, beta_all, 0.0), axis=-1, keepdims=True)
        gc = jnp.sum(jnp.where(lane == h + DN_HEADS, gc_all, 0.0), axis=-1, keepdims=True)
        gc_row = jnp.sum(jnp.where(eye, gc, 0.0), axis=0, keepdims=True)
        gc_last = gc[c - 1:c, :]
        decay = jnp.exp(jnp.where(causal, gc - gc_row, -jnp.inf))
        egc = jnp.exp(gc)
        kb = k * beta
        m = jnp.where(strict, _dot_nt(kb, k, HIGHEST) * decay, 0.0)
        t_inv = _unit_lower_inverse(m)
        w_c = _dot_hi(t_inv, kb * egc)
        u_c = _dot_hi(t_inv, v * beta)
        attn = _dot_nt(q, k, HIGHEST) * decay
        q_dec = q * egc
        k_dec = k * jnp.exp(gc_last - gc)
        v_new = u_c - _dot_hi(w_c, s)
        o = _dot_hi(q_dec, s) + _dot_hi(attn, v_new)
        s = s * jnp.exp(gc_last) + _dot_tn(k_dec, v_new, HIGHEST)
        o = o * lax.rsqrt(jnp.mean(o * o, axis=-1, keepdims=True) + EPS) * nw
        o_ref[0, pl.ds(t0, c), :] = (o * _silu(z_ref[0, pl.ds(t0, c), :])).astype(o_ref.dtype)
        return s

    s_ref[0, 0] = lax.fori_loop(0, n_chunks, chunk, jnp.zeros((DN_DK, DN_DV), F32))


def _dn_prompt(proj, ba, w_short, gp, dn_norm):
    b, t, _ = proj.shape
    hq = DN_HEADS

    def col(off):
        return pl.BlockSpec((1, t, LANES), lambda bi, hi: (bi, 0, off + hi))

    def wcol(off):
        return pl.BlockSpec((SHORT_CONV, LANES), lambda bi, hi: (0, off + hi))

    return pl.pallas_call(
        _dn_prompt_kernel,
        grid=(b, hq),
        in_specs=[
            col(0), col(hq), col(2 * hq), col(3 * hq),
            pl.BlockSpec((1, t, LANES), lambda bi, hi: (bi, 0, 0)),
            wcol(0), wcol(hq), wcol(2 * hq),
            pl.BlockSpec((SUBLANES, LANES), lambda bi, hi: (0, 0)),
            pl.BlockSpec((1, LANES), lambda bi, hi: (0, 0)),
        ],
        out_specs=[
            pl.BlockSpec((1, t, LANES), lambda bi, hi: (bi, 0, hi)),
            pl.BlockSpec((1, 1, DN_DK, DN_DV), lambda bi, hi: (bi, hi, 0, 0)),
        ],
        out_shape=[
            jax.ShapeDtypeStruct((b, t, DN_VW), BF16),
            jax.ShapeDtypeStruct((b, hq, DN_DK, DN_DV), F32),
        ],
        compiler_params=_params("parallel", "parallel"),
        name="dn_prompt",
    )(proj, proj, proj, proj, ba, w_short, w_short, w_short, gp, dn_norm.reshape(1, DN_DV))


def _dn_sample_kernel(x_ref, st_ref, z_ref, ba_ref, w_ref, gp_ref, nw_ref, s_ref,
                      o_ref, sto_ref, so_ref, o_scr):
    hq = DN_HEADS
    w = w_ref[...]
    x_new = x_ref[0]
    y = w[3] * x_new + w[2] * st_ref[0, 2] + w[1] * st_ref[0, 1] + w[0] * st_ref[0, 0]
    y = _silu(y)
    sto_ref[0, 0] = st_ref[0, 1]
    sto_ref[0, 1] = st_ref[0, 2]
    sto_ref[0, 2] = x_new

    def l2n(a):
        return a * lax.rsqrt(jnp.sum(a * a, axis=-1, keepdims=True) + EPS)

    q = l2n(y[0:hq]) * (DN_DK ** -0.5)
    k = l2n(y[hq:2 * hq])
    v = y[2 * hq:3 * hq]
    beta_all, g_all = _gates(ba_ref[0], gp_ref[...])
    r = lax.broadcasted_iota(jnp.int32, (DN_DK, LANES), 0)
    ln = lax.broadcasted_iota(jnp.int32, (DN_DK, LANES), 1)
    eye = r == ln
    lane1 = lax.broadcasted_iota(jnp.int32, (1, LANES), 1)

    for hh in range(hq):
        beta = jnp.sum(jnp.where(lane1 == hh, beta_all, 0.0), axis=-1, keepdims=True)
        g = jnp.sum(jnp.where(lane1 == hh + hq, g_all, 0.0), axis=-1, keepdims=True)
        k_col = jnp.sum(jnp.where(eye, k[hh:hh + 1], 0.0), axis=1, keepdims=True)
        q_col = jnp.sum(jnp.where(eye, q[hh:hh + 1], 0.0), axis=1, keepdims=True)
        s = s_ref[0, hh] * jnp.exp(g)
        delta = (v[hh:hh + 1] - jnp.sum(s * k_col, axis=0, keepdims=True)) * beta
        s = s + k_col * delta
        so_ref[0, hh] = s
        o_scr[hh:hh + 1, :] = jnp.sum(s * q_col, axis=0, keepdims=True)

    o = o_scr[...]
    o = o * lax.rsqrt(jnp.mean(o * o, axis=-1, keepdims=True) + EPS) * nw_ref[...]
    o_ref[0] = (o * _silu(z_ref[0])).astype(o_ref.dtype)


def _dn_sample(proj_rows, ba, conv_state, w_short, gp, dn_norm, ssm_state):
    b = proj_rows.shape[0]
    hq = DN_HEADS
    rows = 3 * hq
    return pl.pallas_call(
        _dn_sample_kernel,
        grid=(b,),
        in_specs=[
            pl.BlockSpec((1, rows, LANES), lambda i: (i, 0, 0)),
            pl.BlockSpec((1, SHORT_CONV - 1, rows, LANES), lambda i: (i, 0, 0, 0)),
            pl.BlockSpec((1, hq, LANES), lambda i: (i, 3, 0)),
            pl.BlockSpec((1, 1, LANES), lambda i: (i, 0, 0)),
            pl.BlockSpec((SHORT_CONV, rows, LANES), lambda i: (0, 0, 0)),
            pl.BlockSpec((SUBLANES, LANES), lambda i: (0, 0)),
            pl.BlockSpec((1, LANES), lambda i: (0, 0)),
            pl.BlockSpec((1, hq, DN_DK, DN_DV), lambda i: (i, 0, 0, 0)),
        ],
        out_specs=[
            pl.BlockSpec((1, hq, LANES), lambda i: (i, 0, 0)),
            pl.BlockSpec((1, SHORT_CONV - 1, rows, LANES), lambda i: (i, 0, 0, 0)),
            pl.BlockSpec((1, hq, DN_DK, DN_DV), lambda i: (i, 0, 0, 0)),
        ],
        out_shape=[
            jax.ShapeDtypeStruct((b, hq, DN_DV), BF16),
            jax.ShapeDtypeStruct((b, SHORT_CONV - 1, rows, LANES), F32),
            jax.ShapeDtypeStruct((b, hq, DN_DK, DN_DV), F32),
        ],
        scratch_shapes=[pltpu.VMEM((hq, DN_DV), F32)],
        compiler_params=_params("parallel"),
        name="dn_sample",
    )(proj_rows, conv_state, proj_rows, ba.reshape(b, 1, LANES), w_short.reshape(SHORT_CONV, rows, LANES), gp,
      dn_norm.reshape(1, DN_DV), ssm_state)


CF_HALO = 32


def _cf_conv_kernel(a_ref, b_ref, ah_ref, bh_ref, w_ref, bias_ref, y_ref, st_ref, scr, *, row_tile):
    i = pl.program_id(2)
    tt = a_ref.shape[1]
    glu = a_ref[0] * jax.nn.sigmoid(b_ref[0])
    keep = jnp.where(i > 0, 1.0, 0.0).astype(F32)
    scr[0:CF_HALO] = ah_ref[0] * jax.nn.sigmoid(bh_ref[0]) * keep
    scr[CF_HALO:CF_HALO + tt] = glu
    w = w_ref[...]
    bias = bias_ref[...]
    for r0 in range(0, tt, row_tile):
        acc = jnp.broadcast_to(bias, (row_tile, bias.shape[1]))
        for j in range(CF_WIDTH):
            start = r0 + CF_HALO - (CF_WIDTH - 1) + j
            acc = acc + w[j:j + 1] * scr[start:start + row_tile]
        y_ref[0, r0:r0 + row_tile] = acc

    @pl.when(i == pl.num_programs(2) - 1)
    def _():
        st_ref[0] = scr[CF_HALO + tt - (CF_WIDTH - 1):CF_HALO + tt]


def _cf_conv_prompt(proj, off_a, off_b, w_cf, b_cf, tt=256, tc=512, row_tile=32):
    b, t, _ = proj.shape
    ch = w_cf.shape[1]
    tt = min(tt, t)
    hb = tt // CF_HALO

    def main(off):
        return pl.BlockSpec((1, tt, tc), lambda bi, j, i: (bi, i, off // tc + j))

    def halo(off):
        return pl.BlockSpec((1, CF_HALO, tc), lambda bi, j, i: (bi, jnp.maximum(i * hb - 1, 0), off // tc + j))

    return pl.pallas_call(
        functools.partial(_cf_conv_kernel, row_tile=row_tile),
        grid=(b, ch // tc, t // tt),
        in_specs=[
            main(off_a), main(off_b), halo(off_a), halo(off_b),
            pl.BlockSpec((CF_WIDTH, tc), lambda bi, j, i: (0, j)),
            pl.BlockSpec((1, tc), lambda bi, j, i: (0, j)),
        ],
        out_specs=[
            pl.BlockSpec((1, tt, tc), lambda bi, j, i: (bi, i, j)),
            pl.BlockSpec((1, CF_WIDTH - 1, tc), lambda bi, j, i: (bi, 0, j)),
        ],
        out_shape=[
            jax.ShapeDtypeStruct((b, t, ch), F32),
            jax.ShapeDtypeStruct((b, CF_WIDTH - 1, ch), F32),
        ],
        scratch_shapes=[pltpu.VMEM((CF_HALO + tt, tc), F32)],
        compiler_params=_params("parallel", "parallel", "arbitrary"),
        name="cf_conv",
    )(proj, proj, proj, proj, w_cf, b_cf.reshape(1, ch))


def _ln_silu(y, g, b):
    mu = jnp.mean(y, axis=-1, keepdims=True)
    yc = y - mu
    out = yc * lax.rsqrt(jnp.mean(yc * yc, axis=-1, keepdims=True) + EPS) * g + b
    return _silu(out)


def _cf_ln_kernel(y_ref, g_ref, b_ref, o_ref):
    o_ref[0] = _ln_silu(y_ref[0], g_ref[...], b_ref[...]).astype(o_ref.dtype)


def _cf_ln(y, g, b, tt=256):
    bsz, t, ch = y.shape
    tt = min(tt, t)
    spec = pl.BlockSpec((1, tt, ch), lambda bi, i: (bi, i, 0))
    vec = pl.BlockSpec((1, ch), lambda bi, i: (0, 0))
    return pl.pallas_call(
        _cf_ln_kernel,
        grid=(bsz, t // tt),
        in_specs=[spec, vec, vec],
        out_specs=spec,
        out_shape=jax.ShapeDtypeStruct((bsz, t, ch), BF16),
        compiler_params=_params("parallel", "parallel"),
        name="cf_ln",
    )(y, g.reshape(1, ch), b.reshape(1, ch))


def _cf_sample_kernel(a_ref, b_ref, st_ref, w_ref, bias_ref, g_ref, be_ref, o_ref, so_ref):
    nb = a_ref.shape[0]
    hist = CF_WIDTH - 1
    w = w_ref[...]
    for e in range(nb):
        glu = a_ref[e:e + 1, :] * jax.nn.sigmoid(b_ref[e:e + 1, :])
        st = st_ref[e]
        y = jnp.sum(w[0:hist] * st, axis=0, keepdims=True) + w[hist:hist + 1] * glu + bias_ref[...]
        o_ref[e:e + 1, :] = _ln_silu(y, g_ref[...], be_ref[...]).astype(o_ref.dtype)
        so_ref[e, 0:hist - 1, :] = st[1:hist]
        so_ref[e, hist - 1:hist, :] = glu


def _cf_sample(proj2d, off_a, off_b, state, w_cf, b_cf, g, be, nb=8):
    bsz = proj2d.shape[0]
    ch = w_cf.shape[1]
    nb = min(nb, bsz)
    vec = pl.BlockSpec((1, ch), lambda i: (0, 0))
    st_spec = pl.BlockSpec((nb, CF_WIDTH - 1, ch), lambda i: (i, 0, 0))
    return pl.pallas_call(
        _cf_sample_kernel,
        grid=(bsz // nb,),
        in_specs=[
            pl.BlockSpec((nb, ch), lambda i: (i, off_a // ch)),
            pl.BlockSpec((nb, ch), lambda i: (i, off_b // ch)),
            st_spec,
            pl.BlockSpec((CF_WIDTH, ch), lambda i: (0, 0)),
            vec, vec, vec,
        ],
        out_specs=[pl.BlockSpec((nb, ch), lambda i: (i, 0)), st_spec],
        out_shape=[
            jax.ShapeDtypeStruct((bsz, ch), BF16),
            jax.ShapeDtypeStruct((bsz, CF_WIDTH - 1, ch), F32),
        ],
        compiler_params=_params("parallel"),
        name="cf_sample",
    )(proj2d, proj2d, state, w_cf, b_cf.reshape(1, ch), g.reshape(1, ch), be.reshape(1, ch))


def _top_values(x, n, out_ref):
    for i in range(n):
        m = jnp.max(x, axis=0, keepdims=True)
        out_ref[i:i + 1, :] = m
        x = jnp.where(x == m, -jnp.inf, x)


def _peer_score_kernel(q_ref, keys_ref, s1_ref, e1_ref, s2_ref, e2_ref, thr_ref, v1_scr, v2_scr, c_scr):
    kk = PEER_TOPK
    for hh in range(PEER_HEADS):
        base = hh * 2 * PEER_HALF
        s1 = _dot_nt(keys_ref[hh, 0], q_ref[:, base:base + PEER_HALF], HIGHEST)
        s2 = _dot_nt(keys_ref[hh, 1], q_ref[:, base + PEER_HALF:base + 2 * PEER_HALF], HIGHEST)
        _top_values(s1, kk, v1_scr)
        _top_values(s2, kk, v2_scr)
        v2 = v2_scr[...]
        for i in range(kk):
            c_scr[i * kk:(i + 1) * kk, :] = v1_scr[i:i + 1, :] + v2
        cand = c_scr[...]
        x = cand
        for _ in range(kk - 1):
            m = jnp.max(x, axis=0, keepdims=True)
            x = jnp.where(x == m, -jnp.inf, x)
        thr = jnp.max(x, axis=0, keepdims=True)
        m1 = v1_scr[0:1, :]
        m2 = v2_scr[0:1, :]
        z = jnp.sum(jnp.where(cand >= thr, jnp.exp(cand - (m1 + m2)), 0.0), axis=0, keepdims=True)
        s1_ref[hh] = s1
        s2_ref[hh] = s2
        e1_ref[hh] = jnp.exp(s1 - m1)
        e2_ref[hh] = jnp.exp(s2 - m2) / z
        thr_ref[hh:hh + 1, :] = thr


def _peer_scores(q2d, sub_keys, tm):
    m = q2d.shape[0]
    big = jax.ShapeDtypeStruct((PEER_HEADS, PEER_NKEYS, m), F32)
    big_spec = pl.BlockSpec((PEER_HEADS, PEER_NKEYS, tm), lambda i: (0, 0, i))
    return pl.pallas_call(
        _peer_score_kernel,
        grid=(m // tm,),
        in_specs=[
            pl.BlockSpec((tm, q2d.shape[1]), lambda i: (i, 0)),
            pl.BlockSpec(sub_keys.shape, lambda i: (0, 0, 0, 0)),
        ],
        out_specs=[big_spec, big_spec, big_spec, big_spec, pl.BlockSpec((PEER_HEADS, tm), lambda i: (0, i))],
        out_shape=[big, big, big, big, jax.ShapeDtypeStruct((PEER_HEADS, m), F32)],
        scratch_shapes=[
            pltpu.VMEM((PEER_TOPK, tm), F32),
            pltpu.VMEM((PEER_TOPK, tm), F32),
            pltpu.VMEM((PEER_TOPK * PEER_TOPK, tm), F32),
        ],
        compiler_params=_params("parallel"),
        name="peer_scores",
    )(q2d, sub_keys)


def _peer_main_kernel(h_ref, u_ref, v_ref, s1_ref, e1_ref, s2_ref, e2_ref, thr_ref, o_ref, *, n_a):
    @pl.when(pl.program_id(1) == 0)
    def _():
        o_ref[...] = jnp.zeros_like(o_ref)

    at = _dot_nt(u_ref[...], h_ref[...])
    act = 0.5 * at * (1.0 + lax.erf(at * SQRT_HALF))
    parts = []
    for ai in range(n_a):
        g = jnp.zeros((PEER_NKEYS, at.shape[1]), F32)
        for hh in range(PEER_HEADS):
            hit = s1_ref[hh, 0, ai:ai + 1, :] + s2_ref[hh] >= thr_ref[hh:hh + 1, :]
            g = g + jnp.where(hit, e1_ref[hh, 0, ai:ai + 1, :] * e2_ref[hh], 0.0)
        parts.append(g * act[ai * PEER_NKEYS:(ai + 1) * PEER_NKEYS])
    coef = jnp.concatenate(parts, axis=0).astype(BF16)
    o_ref[...] += _dot_tn(coef, v_ref[...])


def _peer_main(h2d, u_bf, v_bf, s1, e1, s2, e2, thr, tm, n_a=4):
    m, d = h2d.shape
    n_exp = u_bf.shape[0]
    te = n_a * PEER_NKEYS
    groups = PEER_NKEYS // n_a
    s1g = s1.reshape(PEER_HEADS, groups, n_a, m)
    e1g = e1.reshape(PEER_HEADS, groups, n_a, m)
    row_spec = pl.BlockSpec((PEER_HEADS, 1, n_a, tm), lambda i, e: (0, e, 0, i))
    full_spec = pl.BlockSpec((PEER_HEADS, PEER_NKEYS, tm), lambda i, e: (0, 0, i))
    return pl.pallas_call(
        functools.partial(_peer_main_kernel, n_a=n_a),
        grid=(m // tm, n_exp // te),
        in_specs=[
            pl.BlockSpec((tm, d), lambda i, e: (i, 0)),
            pl.BlockSpec((te, d), lambda i, e: (e, 0)),
            pl.BlockSpec((te, d), lambda i, e: (e, 0)),
            row_spec, row_spec, full_spec, full_spec,
            pl.BlockSpec((PEER_HEADS, tm), lambda i, e: (0, i)),
        ],
        out_specs=pl.BlockSpec((tm, d), lambda i, e: (i, 0)),
        out_shape=jax.ShapeDtypeStruct((m, d), F32),
        compiler_params=_params("parallel", "arbitrary"),
        name="peer_main",
    )(h2d, u_bf, v_bf, s1g, e1g, s2, e2, thr)


def _prep_layer_weights(l, w_in, w_short_conv, a_log, dt_bias, w_out, peer_w_q, peer_u, peer_v):
    off_z = DN_CONV_CH
    off_beta = off_z + DN_VW
    off_glu = off_beta + 2 * DN_HEADS
    w = w_in[l]
    w_main = jnp.concatenate([w[:, :off_beta], w[:, off_glu:]], axis=1).astype(BF16)
    w_ba = jnp.pad(w[:, off_beta:off_glu], ((0, 0), (0, LANES - 2 * DN_HEADS))).astype(BF16)
    gp = jnp.zeros((SUBLANES, LANES), F32)
    gp = gp.at[0, DN_HEADS:2 * DN_HEADS].set(a_log[l]).at[1, DN_HEADS:2 * DN_HEADS].set(dt_bias[l])
    return dict(w_main=w_main, w_ba=w_ba, gp=gp, w_out=w_out[l].astype(BF16), w_q=peer_w_q[l].astype(BF16),
                u=peer_u[l].astype(BF16), v=peer_v[l].astype(BF16), w_short=w_short_conv[l])


def _peer(h, lw, sub_keys, tm):
    g, t, d = h.shape
    m = g * t
    q = _matmul(h, lw["w_q"])
    s1, e1, s2, e2, thr = _peer_scores(q.reshape(m, -1), sub_keys, tm=min(tm, 256))
    out = _peer_main(h.reshape(m, d), lw["u"], lw["v"], s1, e1, s2, e2, thr, tm=tm)
    return out.reshape(g, t, d)


def kernel(x_prompt, x_sample, state_dn_conv, state_dn_ssm, state_cf_conv, c_prompt, c_sample, norm_mix,
           norm_ffn, norm_final, w_ada, b_ada, w_in, w_short_conv, a_log, dt_bias, dn_norm, w_cf_dw, b_cf_dw,
           cf_ln_g, cf_ln_b, w_out, peer_w_q, peer_sub_keys, peer_u, peer_v):
    depth = w_in.shape[0]
    bp, t, d = x_prompt.shape
    bs = x_sample.shape[0]
    glu_a = DN_CONV_CH + DN_VW
    glu_b = glu_a + d // 2

    n_c = bp + bs
    r_pad = -n_c % SUBLANES
    c_all = jnp.pad(jnp.concatenate([c_prompt, c_sample], axis=0), ((0, r_pad), (0, 0)))
    mod_all = _ada(c_all, w_ada, b_ada)

    xp = x_prompt
    xs = x_sample.reshape(1, bs, d)
    pending_p = pending_s = None
    outs = {k: [] for k in ("dn_conv_p", "dn_ssm_p", "cf_conv_p", "dn_conv_s", "dn_ssm_s", "cf_conv_s")}
    tm_p = min(512, bp * t)

    for l in range(depth):
        lw = _prep_layer_weights(l, w_in, w_short_conv, a_log, dt_bias, w_out, peer_w_q, peer_u, peer_v)
        mod_p = mod_all[l, :bp][:, None, :]
        mod_s = mod_all[l, bp:n_c][None]

        if pending_p is None:
            h = _norm(xp, norm_mix[l], mod_p, (1, 0))
        else:
            xp, h = _norm(xp, norm_mix[l], mod_p, (1, 0), res=pending_p[0], res_mod=pending_p[1], res_gate_col=5)
        proj = _matmul(h, lw["w_main"])
        ba = _matmul(h, lw["w_ba"], tn=LANES)
        o_dn, dn_s = _dn_prompt(proj, ba, lw["w_short"], lw["gp"], dn_norm[l])
        y_cf, cf_buf = _cf_conv_prompt(proj, glu_a, glu_b, w_cf_dw[l], b_cf_dw[l])
        o_cf = _cf_ln(y_cf, cf_ln_g[l], cf_ln_b[l])
        outs["dn_conv_p"].append(proj[:, t - (SHORT_CONV - 1):, :DN_CONV_CH])
        outs["dn_ssm_p"].append(dn_s)
        outs["cf_conv_p"].append(cf_buf)
        xp = _matmul_out(o_dn, o_cf, lw["w_out"], xp, mod_p, 2)
        h2 = _norm(xp, norm_ffn[l], mod_p, (4, 3))
        pending_p = (_peer(h2, lw, peer_sub_keys[l], tm_p), mod_p)

        if pending_s is None:
            hs = _norm(xs, norm_mix[l], mod_s, (1, 0))
        else:
            xs, hs = _norm(xs, norm_mix[l], mod_s, (1, 0), res=pending_s[0], res_mod=pending_s[1], res_gate_col=5)
        proj_s = _matmul(hs, lw["w_main"])
        ba_s = _matmul(hs, lw["w_ba"], tn=LANES)
        rows = proj_s.reshape(bs, -1, LANES)
        conv_state = state_dn_conv[l].reshape(bs, SHORT_CONV - 1, 3 * DN_HEADS, LANES)
        o_dn_s, conv_new, ssm_new = _dn_sample(rows, ba_s.reshape(bs, LANES), conv_state, lw["w_short"],
                                               lw["gp"], dn_norm[l], state_dn_ssm[l])
        o_cf_s, cf_new = _cf_sample(proj_s.reshape(bs, -1), glu_a, glu_b, state_cf_conv[l], w_cf_dw[l],
                                    b_cf_dw[l], cf_ln_g[l], cf_ln_b[l])
        outs["dn_conv_s"].append(conv_new.reshape(bs, SHORT_CONV - 1, DN_CONV_CH))
        outs["dn_ssm_s"].append(ssm_new)
        outs["cf_conv_s"].append(cf_new)
        xs = _matmul_out(o_dn_s.reshape(1, bs, DN_VW), o_cf_s.reshape(1, bs, -1), lw["w_out"], xs, mod_s, 2)
        hs2 = _norm(xs, norm_ffn[l], mod_s, (4, 3))
        pending_s = (_peer(hs2, lw, peer_sub_keys[l], bs), mod_s)

    _, y_p = _norm(xp, norm_final, res=pending_p[0], res_mod=pending_p[1], res_gate_col=5, out_dtype=F32)
    _, y_s = _norm(xs, norm_final, res=pending_s[0], res_mod=pending_s[1], res_gate_col=5, out_dtype=F32)
    return (y_p, y_s.reshape(bs, 1, d),
            jnp.stack(outs["dn_conv_p"]), jnp.stack(outs["dn_ssm_p"]), jnp.stack(outs["cf_conv_p"]),
            jnp.stack(outs["dn_conv_s"]), jnp.stack(outs["dn_ssm_s"]), jnp.stack(outs["cf_conv_s"]))
```
